```python
import jax, jax.numpy as jnp
from jax import lax
import numpy as np

D_MODEL = 1024
BATCH = 8
SEQ = 4096
DEPTH = 1

PLE_DIM = 256
EPS = 1e-6
CHUNK = 64
GLA_HEADS = 4
GLA_DK = D_MODEL // (2 * GLA_HEADS)
GLA_DV = D_MODEL // GLA_HEADS
GLA_QK = GLA_HEADS * GLA_DK
GLA_V = GLA_HEADS * GLA_DV
GLA_RANK = 16
GLA_GATE_NORM = 16.0
ML_HEADS = 4
ML_DK = D_MODEL // (2 * ML_HEADS)
ML_DV = D_MODEL // ML_HEADS
ML_QK = ML_HEADS * ML_DK
ML_V = ML_HEADS * ML_DV
ML_CONV = 3
D_FF = 4 * D_MODEL
FFN_CONV = 3
IN_SPLITS = (GLA_QK, GLA_QK, GLA_V, GLA_V, 2 * GLA_RANK, 2 * ML_QK, ML_V, ML_V, 2 * ML_HEADS, 2 * ML_HEADS, D_MODEL, D_MODEL)
D_IN = sum(IN_SPLITS)

kernel_name = 'hybrid_gla_mlstm_bidir_block'


def rms_norm(x, w):
    xf = x.astype(jnp.float32)
    y = xf * lax.rsqrt(jnp.mean(xf * xf, axis=-1, keepdims=True) + EPS)
    return (y * w.astype(jnp.float32)).astype(x.dtype)


def head_rms_norm(o, w, n_heads):
    B_, S_, W_ = o.shape
    oh = o.reshape(B_, S_, n_heads, W_ // n_heads)
    oh = oh * lax.rsqrt(jnp.mean(oh * oh, axis=-1, keepdims=True) + EPS)
    return oh.reshape(B_, S_, W_) * w.astype(jnp.float32)


def dwconv_centred(x, w, b):
    c = x.shape[-1]
    y = lax.conv_general_dilated(x, w[:, None, :].astype(x.dtype), window_strides=(1,), padding='SAME',
                                 dimension_numbers=('NWC', 'WIO', 'NWC'), feature_group_count=c)
    return y + b.astype(x.dtype)


def to_chunks(t):
    B_, S_, H, d = t.shape
    return t.reshape(B_, S_ // CHUNK, CHUNK, H, d).transpose(1, 0, 3, 2, 4)


def gate_chunks(t):
    B_, S_, H = t.shape
    return t.reshape(B_, S_ // CHUNK, CHUNK, H).transpose(1, 0, 3, 2)


def from_chunks(o):
    N_, B_, H, C_, d = o.shape
    return o.transpose(1, 0, 3, 2, 4).reshape(B_, N_ * C_, H * d)


def flip(t):
    return jnp.flip(t, axis=1)


def gla_scan(q, k, v, log_a):
    B_, S_, H, dk = q.shape
    dv = v.shape[-1]
    causal = jnp.tril(jnp.ones((CHUNK, CHUNK), dtype=bool))

    def step(s_prev, xs):
        qc, kc, vc, gc = xs
        b = jnp.cumsum(gc, axis=2)
        qe = qc * jnp.exp(b)
        ke = kc * jnp.exp(-b)
        a = jnp.where(causal, jnp.einsum('bhid,bhjd->bhij', qe, ke), 0.0)
        o = jnp.einsum('bhij,bhjv->bhiv', a, vc) + jnp.einsum('bhid,bhdv->bhiv', qe, s_prev)
        b_end = b[:, :, -1:, :]
        kd = kc * jnp.exp(b_end - b)
        s_new = s_prev * jnp.exp(b_end[:, :, 0, :])[..., None] + jnp.einsum('bhjd,bhjv->bhdv', kd, vc)
        return s_new, o

    s0 = jnp.zeros((B_, H, dk, dv), jnp.float32)
    _, o = lax.scan(step, s0, (to_chunks(q), to_chunks(k), to_chunks(v), to_chunks(log_a)))
    return from_chunks(o)


def mlstm_scan(q, k, v, log_i, log_f):
    B_, S_, H, dk = q.shape
    dv = v.shape[-1]
    causal = jnp.tril(jnp.ones((CHUNK, CHUNK), dtype=bool))

    def step(carry, xs):
        c_st, n_st, m_st = carry
        qc, kc, vc, ic, fc = xs
        F = jnp.cumsum(fc, axis=-1)
        d_log = F[..., :, None] - F[..., None, :] + ic[..., None, :]
        d_log = jnp.where(causal, d_log, -jnp.inf)
        inter_log = F + m_st[..., None]
        m_i = jnp.maximum(inter_log, jnp.max(d_log, axis=-1))
        scores = jnp.einsum('bhid,bhjd->bhij', qc, kc) * jnp.exp(d_log - m_i[..., None])
        inter = jnp.exp(inter_log - m_i)
        num = jnp.einsum('bhij,bhjv->bhiv', scores, vc) + inter[..., None] * jnp.einsum('bhid,bhdv->bhiv', qc, c_st)
        den = jnp.sum(scores, axis=-1) + inter * jnp.einsum('bhid,bhd->bhi', qc, n_st)
        h = num / jnp.maximum(jnp.abs(den), jnp.exp(-m_i))[..., None]
        F_end = F[..., -1]
        end_log = F_end[..., None] - F + ic
        m_new = jnp.maximum(F_end + m_st, jnp.max(end_log, axis=-1))
        decay = jnp.exp(F_end + m_st - m_new)
        kw = kc * jnp.exp(end_log - m_new[..., None])[..., None]
        c_new = decay[..., None, None] * c_st + jnp.einsum('bhjd,bhjv->bhdv', kw, vc)
        n_new = decay[..., None] * n_st + jnp.sum(kw, axis=2)
        return (c_new, n_new, m_new), h

    init = (jnp.zeros((B_, H, dk, dv), jnp.float32), jnp.zeros((B_, H, dk), jnp.float32),
            jnp.full((B_, H), -jnp.inf, jnp.float32))
    _, h = lax.scan(step, init, (to_chunks(q), to_chunks(k), to_chunks(v), gate_chunks(log_i), gate_chunks(log_f)))
    return from_chunks(h)


def hybrid_mixer(h, w_in, w_gla_decay_f, b_gla_decay_f, w_gla_decay_b, b_gla_decay_b, gla_norm,
                 ml_conv_w, ml_conv_b, ml_igate_b, ml_fgate_b, ml_norm, w_out):
    B_, S_, _ = h.shape
    f32 = jnp.float32
    z = h @ w_in
    split_points = np.cumsum(IN_SPLITS)[:-1].tolist()
    gq, gk, gv, gr, glr, mqk, mv, mo, mi, mf, ga, gb = jnp.split(z, split_points, axis=-1)

    def heads(t, n):
        return t.reshape(B_, S_, n, -1).astype(f32)

    q = heads(gq, GLA_HEADS) * (GLA_DK ** -0.5)
    k = heads(gk, GLA_HEADS)
    v = heads(gv, GLA_HEADS)
    lr_f, lr_b = jnp.split(glr, 2, axis=-1)
    la_f = jax.nn.log_sigmoid((lr_f @ w_gla_decay_f + b_gla_decay_f).astype(f32)) / GLA_GATE_NORM
    la_b = jax.nn.log_sigmoid((lr_b @ w_gla_decay_b + b_gla_decay_b).astype(f32)) / GLA_GATE_NORM
    la_f = la_f.reshape(B_, S_, GLA_HEADS, GLA_DK)
    la_b = la_b.reshape(B_, S_, GLA_HEADS, GLA_DK)
    o_a = gla_scan(q, k, v, la_f) + flip(gla_scan(flip(q), flip(k), flip(v), flip(la_b)))
    y_a = head_rms_norm(o_a, gla_norm, GLA_HEADS) * jax.nn.silu(gr.astype(f32))

    qk = jax.nn.silu(dwconv_centred(mqk, ml_conv_w, ml_conv_b))
    mq, mk = jnp.split(qk, 2, axis=-1)
    q = heads(mq, ML_HEADS)
    k = heads(mk, ML_HEADS) * (ML_DK ** -0.5)
    v = heads(mv, ML_HEADS)
    i_pre = (mi + ml_igate_b).astype(f32)
    lf = jax.nn.log_sigmoid((mf + ml_fgate_b).astype(f32))
    i_f, i_b = jnp.split(i_pre, 2, axis=-1)
    lf_f, lf_b = jnp.split(lf, 2, axis=-1)
    h_b = mlstm_scan(q, k, v, i_f, lf_f) + flip(mlstm_scan(flip(q), flip(k), flip(v), flip(i_b), flip(lf_b)))
    y_b = head_rms_norm(h_b, ml_norm, ML_HEADS) * jax.nn.sigmoid(mo.astype(f32))

    y = jax.nn.sigmoid(ga.astype(f32)) * y_a + jax.nn.sigmoid(gb.astype(f32)) * y_b
    return y.astype(h.dtype) @ w_out


def conv_ffn(h, w_up, ffn_conv_w, ffn_conv_b, w_down):
    u = dwconv_centred(h @ w_up, ffn_conv_w, ffn_conv_b)
    gate, val = jnp.split(u, 2, axis=-1)
    return (jax.nn.gelu(gate, approximate=True) * val) @ w_down


def setup_inputs(seed: int = 0) -> dict:
    key = jax.random.key(seed)
    ks = iter(jax.random.split(key, 32))
    L = DEPTH

    def nrm(shape, scale):
        return scale * jax.random.normal(next(ks), shape, jnp.float32)

    def gain(shape):
        return 1.0 + 0.05 * jax.random.normal(next(ks), shape, jnp.float32)

    fbias = jnp.tile(jnp.linspace(3.0, 6.0, ML_HEADS, dtype=jnp.float32), 2)
    return {
        'x': nrm((BATCH, SEQ, D_MODEL), 1.0),
        'p': nrm((DEPTH, BATCH, SEQ, PLE_DIM), 1.0),
        'norm_mix_pre': gain((L, D_MODEL)),
        'norm_mix_post': gain((L, D_MODEL)),
        'w_in': nrm((L, D_MODEL, D_IN), D_MODEL ** -0.5),
        'w_gla_decay_f': nrm((L, GLA_RANK, GLA_QK), GLA_RANK ** -0.5),
        'b_gla_decay_f': nrm((L, GLA_QK), 0.1),
        'w_gla_decay_b': nrm((L, GLA_RANK, GLA_QK), GLA_RANK ** -0.5),
        'b_gla_decay_b': nrm((L, GLA_QK), 0.1),
        'gla_norm': gain((L, GLA_V)),
        'ml_conv_w': nrm((L, ML_CONV, 2 * ML_QK), ML_CONV ** -0.5),
        'ml_conv_b': nrm((L, 2 * ML_QK), 0.02),
        'ml_igate_b': nrm((L, 2 * ML_HEADS), 0.1),
        'ml_fgate_b': fbias[None, :] + nrm((L, 2 * ML_HEADS), 0.1),
        'ml_norm': gain((L, ML_V)),
        'w_out': nrm((L, D_MODEL, D_MODEL), D_MODEL ** -0.5),
        'norm_ffn_pre': gain((L, D_MODEL)),
        'norm_ffn_post': gain((L, D_MODEL)),
        'w_up': nrm((L, D_MODEL, 2 * D_FF), D_MODEL ** -0.5),
        'ffn_conv_w': nrm((L, FFN_CONV, 2 * D_FF), FFN_CONV ** -0.5),
        'ffn_conv_b': nrm((L, 2 * D_FF), 0.02),
        'w_down': nrm((L, D_FF, D_MODEL), D_FF ** -0.5),
        'w_ple_gate': nrm((L, D_MODEL, D_MODEL), D_MODEL ** -0.5),
        'w_ple_proj': nrm((L, PLE_DIM, D_MODEL), PLE_DIM ** -0.5),
        'norm_ple_post': gain((L, D_MODEL)),
    }


def reference(x, p, norm_mix_pre, norm_mix_post, w_in, w_gla_decay_f, b_gla_decay_f, w_gla_decay_b, b_gla_decay_b,
              gla_norm, ml_conv_w, ml_conv_b, ml_igate_b, ml_fgate_b, ml_norm, w_out, norm_ffn_pre, norm_ffn_post,
              w_up, ffn_conv_w, ffn_conv_b, w_down, w_ple_gate, w_ple_proj, norm_ple_post):
    for i in range(DEPTH):
        h = rms_norm(x, norm_mix_pre[i])
        mix = hybrid_mixer(h, w_in[i], w_gla_decay_f[i], b_gla_decay_f[i], w_gla_decay_b[i], b_gla_decay_b[i],
                           gla_norm[i], ml_conv_w[i], ml_conv_b[i], ml_igate_b[i], ml_fgate_b[i], ml_norm[i], w_out[i])
        x = x + rms_norm(mix, norm_mix_post[i])
        h = rms_norm(x, norm_ffn_pre[i])
        x = x + rms_norm(conv_ffn(h, w_up[i], ffn_conv_w[i], ffn_conv_b[i], w_down[i]), norm_ffn_post[i])
        gate = jax.nn.sigmoid(x @ w_ple_gate[i])
        e = p[i].astype(x.dtype) @ w_ple_proj[i]
        x = x + rms_norm(gate * e, norm_ple_post[i])
    return x
```

```python
import functools

import jax
import jax.numpy as jnp
from jax import lax
from jax.experimental import pallas as pl
from jax.experimental.pallas import tpu as pltpu

EPS = 1e-6
HEADS = 4
GLA_RANK = 16
GLA_GATE_NORM = 16.0
GLA_CHUNK = 64
ML_CHUNK = 128
HALO = 16
SMALL_W = 128
VMEM_LIMIT = 56 * 1024 * 1024

F32 = jnp.float32
BF16 = jnp.bfloat16

_NT = (((1,), (1,)), ((), ()))
_TN = (((0,), (0,)), ((), ()))


def _dot(a, b):
    return jnp.dot(a, b, preferred_element_type=F32)


def _dot_nt(a, b):
    return lax.dot_general(a, b, _NT, preferred_element_type=F32)


def _dot_tn(a, b):
    return lax.dot_general(a, b, _TN, preferred_element_type=F32)


def _rms(x, w):
    return x * lax.rsqrt(jnp.mean(x * x, axis=-1, keepdims=True) + EPS) * w


def _sigmoid(x):
    return 1.0 / (1.0 + jnp.exp(-x))


def _log_sigmoid(x):
    return jnp.minimum(x, 0.0) - jnp.log(1.0 + jnp.exp(-jnp.abs(x)))


def _cumsum_rows(x, reverse):
    n = x.shape[0]
    row = lax.broadcasted_iota(jnp.int32, x.shape, 0)
    s = 1
    while s < n:
        if reverse:
            x = x + jnp.where(row < n - s, pltpu.roll(x, n - s, axis=0), 0.0)
        else:
            x = x + jnp.where(row >= s, pltpu.roll(x, s, axis=0), 0.0)
        s *= 2
    return x


def _in_proj_kernel(x_ref, nw_ref, wbig_ref, wsmall_ref, zbig_ref, zsmall_ref, h_scr):
    @pl.when(pl.program_id(1) == 0)
    def _():
        h = _rms(x_ref[...], nw_ref[...]).astype(BF16)
        h_scr[...] = h
        zsmall_ref[...] = _dot(h, wsmall_ref[...])

    zbig_ref[...] = _dot(h_scr[...], wbig_ref[...]).astype(BF16)


def _in_proj(x2d, nw, w_big, w_small, tm, tn):
    t, d = x2d.shape
    nbig = w_big.shape[1]
    return pl.pallas_call(
        _in_proj_kernel,
        grid=(t // tm, nbig // tn),
        in_specs=[
            pl.BlockSpec((tm, d), lambda i, j: (i, 0)),
            pl.BlockSpec((1, d), lambda i, j: (0, 0)),
            pl.BlockSpec((d, tn), lambda i, j: (0, j)),
            pl.BlockSpec((d, SMALL_W), lambda i, j: (0, 0)),
        ],
        out_specs=[
            pl.BlockSpec((tm, tn), lambda i, j: (i, j)),
            pl.BlockSpec((tm, SMALL_W), lambda i, j: (i, 0)),
        ],
        out_shape=[
            jax.ShapeDtypeStruct((t, nbig), BF16),
            jax.ShapeDtypeStruct((t, SMALL_W), F32),
        ],
        scratch_shapes=[pltpu.VMEM((tm, d), BF16)],
        compiler_params=pltpu.CompilerParams(
            dimension_semantics=("arbitrary", "arbitrary"), vmem_limit_bytes=VMEM_LIMIT),
        name="in_proj",
    )(x2d, nw, w_big, w_small)


def _gla_direction(qkv_ref, zs_ref, wd_ref, bd_ref, o_ref, st_ref, reverse, dk, dv):
    c = qkv_ref.shape[0]
    qk_w = HEADS * dk
    z = qkv_ref[...]
    q = z[:, 0:qk_w].astype(F32) * (dk ** -0.5)
    k = z[:, qk_w:2 * qk_w].astype(F32)
    v = z[:, 2 * qk_w:]
    pre = _dot(zs_ref[...].astype(BF16), wd_ref[...]) + bd_ref[...]
    b = _cumsum_rows(_log_sigmoid(pre) * (1.0 / GLA_GATE_NORM), reverse)
    b_end = b[0:1, :] if reverse else b[c - 1:c, :]
    qe = (q * jnp.exp(b)).astype(BF16)
    ke = (k * jnp.exp(-b)).astype(BF16)
    kd = (k * jnp.exp(b_end - b)).astype(BF16)
    dec = jnp.exp(b_end)
    row = lax.broadcasted_iota(jnp.int32, (c, c), 0)
    col = lax.broadcasted_iota(jnp.int32, (c, c), 1)
    mask = (row <= col) if reverse else (row >= col)
    for h in range(HEADS):
        ks = slice(h * dk, (h + 1) * dk)
        vs = slice(h * dv, (h + 1) * dv)
        a = jnp.where(mask, _dot_nt(qe[:, ks], ke[:, ks]), 0.0).astype(BF16)
        st = st_ref[h]
        o = _dot(a, v[:, vs]) + _dot_nt(qe[:, ks], st.astype(BF16))
        o_ref[:, vs] = o.astype(o_ref.dtype)
        st_ref[h] = st * dec[:, ks] + _dot_tn(v[:, vs], kd[:, ks])


def _gla_kernel(qkv_f, qkv_b, zs_f, zs_b, wdf, bdf, wdb, bdb, o_f, o_b, st_f, st_b, *, dk, dv):
    @pl.when(pl.program_id(1) == 0)
    def _():
        st_f[...] = jnp.zeros_like(st_f)
        st_b[...] = jnp.zeros_like(st_b)

    _gla_direction(qkv_f, zs_f, wdf, bdf, o_f, st_f, False, dk, dv)
    _gla_direction(qkv_b, zs_b, wdb, bdb, o_b, st_b, True, dk, dv)


def _gla_scan(zbig, zsmall, wdf, bdf, wdb, bdb, dk, dv):
    b, s, _ = zbig.shape
    c = GLA_CHUNK
    nc = s // c
    qkv_w = HEADS * (2 * dk + dv)
    v_w = HEADS * dv
    fwd = lambda i, n: (i, n, 0)
    bwd = lambda i, n: (i, nc - 1 - n, 0)
    const = lambda i, n: (0, 0)
    return pl.pallas_call(
        functools.partial(_gla_kernel, dk=dk, dv=dv),
        grid=(b, nc),
        in_specs=[
            pl.BlockSpec((None, c, qkv_w), fwd),
            pl.BlockSpec((None, c, qkv_w), bwd),
            pl.BlockSpec((None, c, SMALL_W), fwd),
            pl.BlockSpec((None, c, SMALL_W), bwd),
            pl.BlockSpec((SMALL_W, HEADS * dk), const),
            pl.BlockSpec((1, HEADS * dk), const),
            pl.BlockSpec((SMALL_W, HEADS * dk), const),
            pl.BlockSpec((1, HEADS * dk), const),
        ],
        out_specs=[
            pl.BlockSpec((None, c, v_w), fwd),
            pl.BlockSpec((None, c, v_w), bwd),
        ],
        out_shape=[jax.ShapeDtypeStruct((b, s, v_w), BF16)] * 2,
        scratch_shapes=[pltpu.VMEM((HEADS, dv, dk), F32)] * 2,
        compiler_params=pltpu.CompilerParams(
            dimension_semantics=("arbitrary", "arbitrary"), vmem_limit_bytes=VMEM_LIMIT),
        name="gla_scan",
    )(zbig, zbig, zsmall, zsmall, wdf, bdf, wdb, bdb)


def _conv3_silu(prev_ref, main_ref, next_ref, w_ref, b_ref, has_prev, has_next):
    c = main_ref.shape[0]
    prev = jnp.where(has_prev, prev_ref[...].astype(F32), 0.0)
    nxt = jnp.where(has_next, next_ref[...].astype(F32), 0.0)
    ext = jnp.concatenate([prev, main_ref[...].astype(F32), nxt], axis=0)
    n = ext.shape[0]
    up = pltpu.roll(ext, 1, axis=0)[HALO:HALO + c]
    dn = pltpu.roll(ext, n - 1, axis=0)[HALO:HALO + c]
    y = up * w_ref[0:1, :] + ext[HALO:HALO + c] * w_ref[1:2, :] + dn * w_ref[2:3, :] + b_ref[...]
    return y * _sigmoid(y)


def _mlstm_direction(qk, v_ref, zs_ref, gb_ref, h_ref, ct_ref, n_ref, m_ref, reverse, d, dk, dv):
    c = v_ref.shape[0]
    qk_w = HEADS * dk
    q = qk[:, 0:qk_w]
    k = qk[:, qk_w:] * (dk ** -0.5)
    v = v_ref[...]
    pre = zs_ref[...] + gb_ref[...]
    fsum = _cumsum_rows(_log_sigmoid(pre), reverse)
    pre_t = pre.T
    fsum_t = fsum.T
    row = lax.broadcasted_iota(jnp.int32, (c, c), 0)
    col = lax.broadcasted_iota(jnp.int32, (c, c), 1)
    mask = (row <= col) if reverse else (row >= col)
    i_base = 2 * GLA_RANK + d * HEADS
    f_base = 2 * GLA_RANK + 2 * HEADS + d * HEADS
    for h in range(HEADS):
        ks = slice(h * dk, (h + 1) * dk)
        vs = slice(h * dv, (h + 1) * dv)
        ic, fc = i_base + h, f_base + h
        f_col = fsum[:, fc:fc + 1]
        i_col = pre[:, ic:ic + 1]
        row_b = pre_t[ic:ic + 1, :] - fsum_t[fc:fc + 1, :]
        m_st = m_ref[h]
        d_log = jnp.where(mask, f_col + row_b, -jnp.inf)
        inter_log = f_col + m_st
        m_i = jnp.maximum(inter_log, jnp.max(d_log, axis=-1, keepdims=True))
        qh = q[:, ks]
        qhb = qh.astype(BF16)
        kh = k[:, ks]
        scores = _dot_nt(qhb, kh.astype(BF16)) * jnp.exp(d_log - m_i)
        inter = jnp.exp(inter_log - m_i)
        ct = ct_ref[h]
        nrow = n_ref[h]
        num = _dot(scores.astype(BF16), v[:, vs]) + inter * _dot_nt(qhb, ct.astype(BF16))
        den = (jnp.sum(scores, axis=-1, keepdims=True)
               + inter * jnp.sum(qh * nrow, axis=-1, keepdims=True))
        hh = num / jnp.maximum(jnp.abs(den), jnp.exp(-m_i))
        h_ref[:, vs] = hh.astype(h_ref.dtype)
        f_end = f_col[0:1, :] if reverse else f_col[c - 1:c, :]
        end_log = f_end - f_col + i_col
        m_new = jnp.maximum(f_end + m_st, jnp.max(end_log, axis=0, keepdims=True))
        decay = jnp.exp(f_end + m_st - m_new)
        kw = kh * jnp.exp(end_log - m_new)
        ct_ref[h] = decay * ct + _dot_tn(v[:, vs], kw.astype(BF16))
        n_ref[h] = decay * nrow + jnp.sum(kw, axis=0, keepdims=True)
        m_ref[h] = m_new


def _mlstm_kernel(pf, mf, nf, vf, zf, pb, mb, nb, vb, zb, cw, cb, gb,
                  h_f, h_b, ct_f, n_f, m_f, ct_b, n_b, m_b, *, dk, dv):
    n = pl.program_id(1)
    last = pl.num_programs(1) - 1

    @pl.when(n == 0)
    def _():
        for r in (ct_f, n_f, ct_b, n_b):
            r[...] = jnp.zeros_like(r)
        m_f[...] = jnp.full_like(m_f, -jnp.inf)
        m_b[...] = jnp.full_like(m_b, -jnp.inf)

    qk_f = _conv3_silu(pf, mf, nf, cw, cb, n > 0, n < last)
    _mlstm_direction(qk_f, vf, zf, gb, h_f, ct_f, n_f, m_f, False, 0, dk, dv)
    qk_b = _conv3_silu(pb, mb, nb, cw, cb, n < last, n > 0)
    _mlstm_direction(qk_b, vb, zb, gb, h_b, ct_b, n_b, m_b, True, 1, dk, dv)


def _mlstm_scan(zbig, zsmall, conv_w, conv_b, gate_b, dk, dv, qk_col, v_col):
    b, s, _ = zbig.shape
    c = ML_CHUNK
    nc = s // c
    hb = c // HALO
    qk_w = 2 * HEADS * dk
    v_w = HEADS * dv
    nh = s // HALO

    def chunk(rev):
        return (lambda n: nc - 1 - n) if rev else (lambda n: n)

    specs = []
    for rev in (False, True):
        ch = chunk(rev)
        specs += [
            pl.BlockSpec((None, HALO, qk_w),
                         lambda i, n, ch=ch: (i, jnp.maximum(ch(n) * hb - 1, 0), qk_col)),
            pl.BlockSpec((None, c, qk_w), lambda i, n, ch=ch: (i, ch(n), qk_col)),
            pl.BlockSpec((None, HALO, qk_w),
                         lambda i, n, ch=ch: (i, jnp.minimum((ch(n) + 1) * hb, nh - 1), qk_col)),
            pl.BlockSpec((None, c, v_w), lambda i, n, ch=ch: (i, ch(n), v_col)),
            pl.BlockSpec((None, c, SMALL_W), lambda i, n, ch=ch: (i, ch(n), 0)),
        ]
    const = lambda i, n: (0, 0)
    specs += [
        pl.BlockSpec((3, qk_w), const),
        pl.BlockSpec((1, qk_w), const),
        pl.BlockSpec((1, SMALL_W), const),
    ]
    state = [pltpu.VMEM((HEADS, dv, dk), F32), pltpu.VMEM((HEADS, 1, dk), F32),
             pltpu.VMEM((HEADS, 1, 1), F32)]
    return pl.pallas_call(
        functools.partial(_mlstm_kernel, dk=dk, dv=dv),
        grid=(b, nc),
        in_specs=specs,
        out_specs=[
            pl.BlockSpec((None, c, v_w), lambda i, n: (i, n, 0)),
            pl.BlockSpec((None, c, v_w), lambda i, n: (i, nc - 1 - n, 0)),
        ],
        out_shape=[jax.ShapeDtypeStruct((b, s, v_w), BF16)] * 2,
        scratch_shapes=state + state,
        compiler_params=pltpu.CompilerParams(
            dimension_semantics=("arbitrary", "arbitrary"), vmem_limit_bytes=VMEM_LIMIT),
        name="mlstm_scan",
    )(*([zbig, zbig, zbig, zbig, zsmall] * 2), conv_w, conv_b, gate_b)


def _head_rms(o, w, dv):
    parts = []
    for h in range(HEADS):
        seg = o[:, h * dv:(h + 1) * dv]
        parts.append(seg * lax.rsqrt(jnp.mean(seg * seg, axis=-1, keepdims=True) + EPS))
    return jnp.concatenate(parts, axis=-1) * w


def _merge_kernel(x_ref, oaf, oab, hbf, hbb, gr, mo, ga, gb, gn, mn, pn, wo, out_ref, *, dv):
    o_a = oaf[...].astype(F32) + oab[...].astype(F32)
    h_b = hbf[...].astype(F32) + hbb[...].astype(F32)
    g_r = gr[...].astype(F32)
    y_a = _head_rms(o_a, gn[...], dv) * (g_r * _sigmoid(g_r))
    y_b = _head_rms(h_b, mn[...], dv) * _sigmoid(mo[...].astype(F32))
    y = _sigmoid(ga[...].astype(F32)) * y_a + _sigmoid(gb[...].astype(F32)) * y_b
    mix = _dot(y.astype(BF16), wo[...])
    out_ref[...] = x_ref[...] + _rms(mix, pn[...])


def _merge_out(x2d, o_af, o_ab, h_bf, h_bb, zbig, gla_norm, ml_norm, post_norm, w_out, tm, cols, dv):
    t, d = x2d.shape
    row = lambda i: (i, 0)
    const = lambda i: (0, 0)
    zspec = lambda cidx: pl.BlockSpec((tm, d), lambda i, cidx=cidx: (i, cidx))
    return pl.pallas_call(
        functools.partial(_merge_kernel, dv=dv),
        grid=(t // tm,),
        in_specs=[pl.BlockSpec((tm, d), row)] * 5
        + [zspec(cols["gr"]), zspec(cols["mo"]), zspec(cols["ga"]), zspec(cols["gb"])]
        + [pl.BlockSpec((1, d), const)] * 3
        + [pl.BlockSpec((d, d), const)],
        out_specs=pl.BlockSpec((tm, d), row),
        out_shape=jax.ShapeDtypeStruct((t, d), F32),
        compiler_params=pltpu.CompilerParams(
            dimension_semantics=("arbitrary",), vmem_limit_bytes=VMEM_LIMIT),
        name="merge_out",
    )(x2d, o_af, o_ab, h_bf, h_bb, zbig, zbig, zbig, zbig, gla_norm, ml_norm, post_norm, w_out)


def _gelu_tanh(x):
    return 0.5 * x * (1.0 + jnp.tanh(0.7978845608028654 * (x + 0.044715 * (x * x * x))))


def _ffn_kernel(xp, xm, xn, p_ref, n_pre, n_post, wug, wuv, cwg, cwv, cbg, cbv, wd, wg, wp, n_ple,
                out_ref, h_scr, acc, *, tiles_per_seq):
    i = pl.program_id(0)
    j = pl.program_id(1)
    tm = xm.shape[0]
    n = tm + 2 * HALO

    @pl.when(j == 0)
    def _():
        t_in_seq = i % tiles_per_seq
        nw = n_pre[...]
        hp = jnp.where(t_in_seq > 0, _rms(xp[...], nw), 0.0)
        hn = jnp.where(t_in_seq < tiles_per_seq - 1, _rms(xn[...], nw), 0.0)
        h_scr[0:HALO, :] = hp.astype(BF16)
        h_scr[HALO:HALO + tm, :] = _rms(xm[...], nw).astype(BF16)
        h_scr[HALO + tm:n, :] = hn.astype(BF16)
        acc[...] = jnp.zeros_like(acc)

    h = h_scr[...]

    def conv(w_ref, cw_ref, cb_ref):
        u = _dot(h, w_ref[...])
        up = pltpu.roll(u, 1, axis=0)[HALO:HALO + tm]
        dn = pltpu.roll(u, n - 1, axis=0)[HALO:HALO + tm]
        return (up * cw_ref[0:1, :] + u[HALO:HALO + tm] * cw_ref[1:2, :] + dn * cw_ref[2:3, :]
                + cb_ref[...])

    act = _gelu_tanh(conv(wug, cwg, cbg)) * conv(wuv, cwv, cbv)
    acc[...] += _dot(act.astype(BF16), wd[...])

    @pl.when(j == pl.num_programs(1) - 1)
    def _():
        x2 = xm[...] + _rms(acc[...], n_post[...])
        gate = _sigmoid(_dot(x2.astype(BF16), wg[...]))
        e = _dot(p_ref[...].astype(BF16), wp[...])
        out_ref[...] = x2 + _rms(gate * e, n_ple[...])


def _ffn_ple(x1, p2d, n_pre, n_post, w_up, conv_w, conv_b, w_down, w_gate, w_proj, n_ple, seq, tm, fb):
    t, d = x1.shape
    dff = w_down.shape[0]
    nj = dff // fb
    hb = tm // HALO
    nh = t // HALO
    pd = p2d.shape[1]
    const = lambda i, j: (0, 0)
    row = lambda i, j: (i, 0)
    return pl.pallas_call(
        functools.partial(_ffn_kernel, tiles_per_seq=seq // tm),
        grid=(t // tm, nj),
        in_specs=[
            pl.BlockSpec((HALO, d), lambda i, j: (jnp.maximum(i * hb - 1, 0), 0)),
            pl.BlockSpec((tm, d), row),
            pl.BlockSpec((HALO, d), lambda i, j: (jnp.minimum((i + 1) * hb, nh - 1), 0)),
            pl.BlockSpec((tm, pd), row),
            pl.BlockSpec((1, d), const),
            pl.BlockSpec((1, d), const),
            pl.BlockSpec((d, fb), lambda i, j: (0, j)),
            pl.BlockSpec((d, fb), lambda i, j: (0, nj + j)),
            pl.BlockSpec((3, fb), lambda i, j: (0, j)),
            pl.BlockSpec((3, fb), lambda i, j: (0, nj + j)),
            pl.BlockSpec((1, fb), lambda i, j: (0, j)),
            pl.BlockSpec((1, fb), lambda i, j: (0, nj + j)),
            pl.BlockSpec((fb, d), lambda i, j: (j, 0)),
            pl.BlockSpec((d, d), const),
            pl.BlockSpec((pd, d), const),
            pl.BlockSpec((1, d), const),
        ],
        out_specs=pl.BlockSpec((tm, d), row),
        out_shape=jax.ShapeDtypeStruct((t, d), F32),
        scratch_shapes=[pltpu.VMEM((tm + 2 * HALO, d), BF16), pltpu.VMEM((tm, d), F32)],
        compiler_params=pltpu.CompilerParams(
            dimension_semantics=("arbitrary", "arbitrary"), vmem_limit_bytes=VMEM_LIMIT),
        name="ffn_ple",
    )(x1, x1, x1, p2d, n_pre, n_post, w_up, w_up, conv_w, conv_w, conv_b, conv_b, w_down,
      w_gate, w_proj, n_ple)


def _layer(x, p, norm_mix_pre, norm_mix_post, w_in, w_gla_decay_f, b_gla_decay_f, w_gla_decay_b,
           b_gla_decay_b, gla_norm, ml_conv_w, ml_conv_b, ml_igate_b, ml_fgate_b, ml_norm, w_out,
           norm_ffn_pre, norm_ffn_post, w_up, ffn_conv_w, ffn_conv_b, w_down, w_ple_gate, w_ple_proj,
           norm_ple_post):
    bsz, seq, d = x.shape
    t = bsz * seq
    qk_w = d // 2
    dk = qk_w // HEADS
    dv = d // HEADS
    row = lambda a: a.reshape(1, -1)

    o_glr = 2 * qk_w + 2 * d
    o_mqk = o_glr + 2 * GLA_RANK
    o_mi = o_mqk + 2 * qk_w + 2 * d
    o_ga = o_mi + 4 * HEADS
    w_big = jnp.concatenate([w_in[:, :o_glr], w_in[:, o_mqk:o_mi], w_in[:, o_ga:]], axis=1).astype(BF16)
    n_small = 2 * GLA_RANK + 4 * HEADS
    w_small = jnp.concatenate(
        [w_in[:, o_glr:o_mqk], w_in[:, o_mi:o_ga], jnp.zeros((d, SMALL_W - n_small), F32)],
        axis=1).astype(BF16)
    cols = {"gr": 2, "mo": 5, "ga": 6, "gb": 7}

    x2d = x.reshape(t, d)
    zbig, zsmall = _in_proj(x2d, row(norm_mix_pre), w_big, w_small, tm=1024, tn=1024)
    zbig3 = zbig.reshape(bsz, seq, -1)
    zsmall3 = zsmall.reshape(bsz, seq, SMALL_W)

    def pad_decay(w, off):
        return jnp.zeros((SMALL_W, qk_w), F32).at[off:off + GLA_RANK].set(w).astype(BF16)

    o_af, o_ab = _gla_scan(zbig3, zsmall3, pad_decay(w_gla_decay_f, 0), row(b_gla_decay_f),
                           pad_decay(w_gla_decay_b, GLA_RANK), row(b_gla_decay_b), dk, dv)

    gate_b = jnp.zeros((1, SMALL_W), F32)
    gate_b = gate_b.at[0, 2 * GLA_RANK:2 * GLA_RANK + 2 * HEADS].set(ml_igate_b)
    gate_b = gate_b.at[0, 2 * GLA_RANK + 2 * HEADS:n_small].set(ml_fgate_b)
    h_bf, h_bb = _mlstm_scan(zbig3, zsmall3, ml_conv_w, row(ml_conv_b), gate_b, dk, dv, qk_col=3, v_col=4)

    x1 = _merge_out(x2d, o_af.reshape(t, d), o_ab.reshape(t, d), h_bf.reshape(t, d), h_bb.reshape(t, d),
                    zbig, row(gla_norm), row(ml_norm), row(norm_mix_post), w_out.astype(BF16),
                    tm=512, cols=cols, dv=dv)

    out = _ffn_ple(x1, p.reshape(t, -1), row(norm_ffn_pre), row(norm_ffn_post), w_up.astype(BF16),
                   ffn_conv_w, row(ffn_conv_b), w_down.astype(BF16), w_ple_gate.astype(BF16),
                   w_ple_proj.astype(BF16), row(norm_ple_post), seq=seq, tm=512, fb=512)
    return out.reshape(bsz, seq, d)


def kernel(x, p, norm_mix_pre, norm_mix_post, w_in, w_gla_decay_f, b_gla_decay_f, w_gla_decay_b, b_gla_decay_b, gla_norm, ml_conv_w, ml_conv_b, ml_igate_b, ml_fgate_b, ml_norm, w_out, norm_ffn_pre, norm_ffn_post, w_up, ffn_conv_w, ffn_conv_b, w_down, w_ple_gate, w_ple_proj, norm_ple_post):
    depth = w_in.shape[0]
    for i in range(depth):
        x = _layer(x, p[i], norm_mix_pre[i], norm_mix_post[i], w_in[i], w_gla_decay_f[i], b_gla_decay_f[i],
                   w_gla_decay_b[i], b_gla_decay_b[i], gla_norm[i], ml_conv_w[i], ml_conv_b[i], ml_igate_b[i],
                   ml_fgate_b[i], ml_norm[i], w_out[i], norm_ffn_pre[i], norm_ffn_post[i], w_up[i],
                   ffn_conv_w[i], ffn_conv_b[i], w_down[i], w_ple_gate[i], w_ple_proj[i], norm_ple_post[i])
    return x
```

```python
import functools
import math

import jax
import jax.numpy as jnp
from jax import lax
from jax.experimental import pallas as pl
from jax.experimental.pallas import tpu as pltpu

EPS = 1e-6
HEADS = 4
GLA_RANK = 16
GLA_GATE_NORM = 16.0
GLA_CHUNK = 64
ML_CHUNK = 128
HALO = 16
LANES = 128
SUBLANES = 8
SMALL_W = 2 * LANES
GATE_LANE = 2 * GLA_RANK
VMEM_LIMIT = 56 * 1024 * 1024
LOG2E = math.log2(math.e)

F32 = jnp.float32
BF16 = jnp.bfloat16

_NT = (((1,), (1,)), ((), ()))
_TN = (((0,), (0,)), ((), ()))


def _dot(a, b):
    return jnp.dot(a, b, preferred_element_type=F32)


def _dot_nt(a, b):
    return lax.dot_general(a, b, _NT, preferred_element_type=F32)


def _dot_tn(a, b):
    return lax.dot_general(a, b, _TN, preferred_element_type=F32)


def _rms(x, w):
    return x * lax.rsqrt(jnp.mean(x * x, axis=-1, keepdims=True) + EPS) * w


def _sigmoid(x):
    return 1.0 / (1.0 + jnp.exp2(x * (-LOG2E)))


def _log2_sigmoid(x):
    return jnp.minimum(x, 0.0) * LOG2E - jnp.log2(1.0 + jnp.exp2(jnp.abs(x) * (-LOG2E)))


def _scan_rows(x, op, fill, reverse):
    n = x.shape[0]
    row = lax.broadcasted_iota(jnp.int32, x.shape, 0)
    s = 1
    while s < min(n, SUBLANES):
        if reverse:
            x = op(x, jnp.where(row < n - s, pltpu.roll(x, n - s, axis=0), fill))
        else:
            x = op(x, jnp.where(row >= s, pltpu.roll(x, s, axis=0), fill))
        s *= 2
    while s < n:
        if reverse:
            x = jnp.concatenate([op(x[:n - s], x[s:]), x[n - s:]], axis=0)
        else:
            x = jnp.concatenate([x[:s], op(x[s:], x[:n - s])], axis=0)
        s *= 2
    return x


def _conv3(prev_ref, main_ref, next_ref, w_ref, b_ref, has_prev, has_next):
    c = main_ref.shape[0]
    prev = jnp.where(has_prev, prev_ref[...].astype(F32), 0.0)
    nxt = jnp.where(has_next, next_ref[...].astype(F32), 0.0)
    ext = jnp.concatenate([prev, main_ref[...].astype(F32), nxt], axis=0)
    n = ext.shape[0]
    up = pltpu.roll(ext, 1, axis=0)[HALO:HALO + c]
    dn = pltpu.roll(ext, n - 1, axis=0)[HALO:HALO + c]
    return up * w_ref[0:1, :] + ext[HALO:HALO + c] * w_ref[1:2, :] + dn * w_ref[2:3, :] + b_ref[...]


def _in_proj_kernel(x_ref, nw_ref, wbig_ref, wsmall_ref, zbig_ref, zsmall_ref, h_scr):
    @pl.when(pl.program_id(1) == 0)
    def _():
        h = _rms(x_ref[...], nw_ref[...]).astype(BF16)
        h_scr[...] = h
        zsmall_ref[...] = _dot(h, wsmall_ref[...])

    zbig_ref[...] = _dot(h_scr[...], wbig_ref[...]).astype(BF16)


def _in_proj(x2d, nw, w_big, w_small, tm, tn):
    t, d = x2d.shape
    nbig = w_big.shape[1]
    return pl.pallas_call(
        _in_proj_kernel,
        grid=(t // tm, nbig // tn),
        in_specs=[
            pl.BlockSpec((tm, d), lambda i, j: (i, 0)),
            pl.BlockSpec((1, d), lambda i, j: (0, 0)),
            pl.BlockSpec((d, tn), lambda i, j: (0, j)),
            pl.BlockSpec((d, SMALL_W), lambda i, j: (0, 0)),
        ],
        out_specs=[
            pl.BlockSpec((tm, tn), lambda i, j: (i, j)),
            pl.BlockSpec((tm, SMALL_W), lambda i, j: (i, 0)),
        ],
        out_shape=[
            jax.ShapeDtypeStruct((t, nbig), BF16),
            jax.ShapeDtypeStruct((t, SMALL_W), F32),
        ],
        scratch_shapes=[pltpu.VMEM((tm, d), BF16)],
        compiler_params=pltpu.CompilerParams(
            dimension_semantics=("arbitrary", "arbitrary"), vmem_limit_bytes=VMEM_LIMIT),
        name="in_proj",
    )(x2d, nw, w_big, w_small)


def _qk_conv_kernel(prev_ref, main_ref, next_ref, w_ref, b_ref, out_ref, *, dk):
    n = pl.program_id(1)
    y = _conv3(prev_ref, main_ref, next_ref, w_ref, b_ref, n > 0, n < pl.num_programs(1) - 1)
    y = y * _sigmoid(y)
    half = y.shape[1] // 2
    out_ref[:, :half] = y[:, :half].astype(out_ref.dtype)
    out_ref[:, half:] = (y[:, half:] * (dk ** -0.5)).astype(out_ref.dtype)


def _qk_conv(zbig3, conv_w, conv_b, dk, qk_col, tc):
    b, s, _ = zbig3.shape
    w = conv_w.shape[1]
    hb = tc // HALO
    nh = s // HALO
    const = lambda i, n: (0, 0)
    return pl.pallas_call(
        functools.partial(_qk_conv_kernel, dk=dk),
        grid=(b, s // tc),
        in_specs=[
            pl.BlockSpec((None, HALO, w), lambda i, n: (i, jnp.maximum(n * hb - 1, 0), qk_col)),
            pl.BlockSpec((None, tc, w), lambda i, n: (i, n, qk_col)),
            pl.BlockSpec((None, HALO, w), lambda i, n: (i, jnp.minimum((n + 1) * hb, nh - 1), qk_col)),
            pl.BlockSpec((3, w), const),
            pl.BlockSpec((1, w), const),
        ],
        out_specs=pl.BlockSpec((None, tc, w), lambda i, n: (i, n, 0)),
        out_shape=jax.ShapeDtypeStruct((b, s, w), BF16),
        compiler_params=pltpu.CompilerParams(
            dimension_semantics=("arbitrary", "arbitrary"), vmem_limit_bytes=VMEM_LIMIT),
        name="qk_conv",
    )(zbig3, zbig3, zbig3, conv_w, conv_b)


def _gla_direction(qkv_ref, zs_ref, wd_ref, bd_ref, o_ref, st_ref, reverse, dk, dv):
    c = qkv_ref.shape[0]
    qk_w = HEADS * dk
    pre = _dot(zs_ref[...].astype(BF16), wd_ref[...]) + bd_ref[...]
    b2 = _scan_rows(_log2_sigmoid(pre) * (1.0 / GLA_GATE_NORM), jnp.add, 0.0, reverse)
    b_end = b2[0:1, :] if reverse else b2[c - 1:c, :]
    z = qkv_ref[...]
    qe = (z[:, 0:qk_w].astype(F32) * (jnp.exp2(b2) * (dk ** -0.5))).astype(BF16)
    ke = (z[:, qk_w:2 * qk_w].astype(F32) * jnp.exp2(-b2)).astype(BF16)
    v = z[:, 2 * qk_w:]
    dec = jnp.exp2(b_end)
    row = lax.broadcasted_iota(jnp.int32, (c, c), 0)
    col = lax.broadcasted_iota(jnp.int32, (c, c), 1)
    mask = (row <= col) if reverse else (row >= col)
    for h in range(HEADS):
        ks = slice(h * dk, (h + 1) * dk)
        vs = slice(h * dv, (h + 1) * dv)
        a = jnp.where(mask, _dot_nt(qe[:, ks], ke[:, ks]), 0.0).astype(BF16)
        st = st_ref[h]
        o = _dot(a, v[:, vs]) + _dot_nt(qe[:, ks], st.astype(BF16))
        o_ref[:, vs] = o.astype(o_ref.dtype)
        st_ref[h] = (st + _dot_tn(v[:, vs], ke[:, ks])) * dec[:, ks]


def _gla_kernel(qkv_f, qkv_b, zs_f, zs_b, wdf, bdf, wdb, bdb, o_f, o_b, st_f, st_b, *, dk, dv):
    @pl.when(pl.program_id(1) == 0)
    def _():
        st_f[...] = jnp.zeros_like(st_f)
        st_b[...] = jnp.zeros_like(st_b)

    _gla_direction(qkv_f, zs_f, wdf, bdf, o_f, st_f, False, dk, dv)
    _gla_direction(qkv_b, zs_b, wdb, bdb, o_b, st_b, True, dk, dv)


def _gla_scan(zbig, zsmall, wdf, bdf, wdb, bdb, dk, dv):
    b, s, _ = zbig.shape
    c = GLA_CHUNK
    nc = s // c
    qkv_w = HEADS * (2 * dk + dv)
    v_w = HEADS * dv
    fwd = lambda i, n: (i, n, 0)
    bwd = lambda i, n: (i, nc - 1 - n, 0)
    const = lambda i, n: (0, 0)
    return pl.pallas_call(
        functools.partial(_gla_kernel, dk=dk, dv=dv),
        grid=(b, nc),
        in_specs=[
            pl.BlockSpec((None, c, qkv_w), fwd),
            pl.BlockSpec((None, c, qkv_w), bwd),
            pl.BlockSpec((None, c, LANES), fwd),
            pl.BlockSpec((None, c, LANES), bwd),
            pl.BlockSpec((LANES, HEADS * dk), const),
            pl.BlockSpec((1, HEADS * dk), const),
            pl.BlockSpec((LANES, HEADS * dk), const),
            pl.BlockSpec((1, HEADS * dk), const),
        ],
        out_specs=[
            pl.BlockSpec((None, c, v_w), fwd),
            pl.BlockSpec((None, c, v_w), bwd),
        ],
        out_shape=[jax.ShapeDtypeStruct((b, s, v_w), BF16)] * 2,
        scratch_shapes=[pltpu.VMEM((HEADS, dv, dk), F32)] * 2,
        compiler_params=pltpu.CompilerParams(
            dimension_semantics=("arbitrary", "arbitrary"), vmem_limit_bytes=VMEM_LIMIT),
        name="gla_scan",
    )(zbig, zbig, zsmall, zsmall, wdf, bdf, wdb, bdb)


def _mlstm_direction(qk_ref, v_ref, zs_ref, bi_ref, bf_ref, h_ref, s_ref, m_ref, reverse, d, dk, dv):
    c = v_ref.shape[0]
    qk_w = HEADS * dk
    zs = zs_ref[...]
    pre_i = (zs[:, :LANES] + bi_ref[...]) * LOG2E
    f = _scan_rows(_log2_sigmoid(zs[:, LANES:] + bf_ref[...]), jnp.add, 0.0, reverse)
    r = pre_i - f
    cm = _scan_rows(r, jnp.maximum, -jnp.inf, reverse)
    m_st = m_ref[...]
    inter_log = f + m_st
    m_i = jnp.maximum(inter_log, f + cm)
    col_a = f - m_i
    inter = jnp.exp2(inter_log - m_i)
    em = jnp.exp2(-m_i)
    last = 0 if reverse else c - 1
    f_end = f[last:last + 1, :]
    m_new = f_end + jnp.maximum(m_st, cm[last:last + 1, :])
    decay = jnp.exp2(f_end + m_st - m_new)
    kw_scale = jnp.exp2(f_end + r - m_new)
    m_ref[...] = m_new
    r_t = r.T

    row = lax.broadcasted_iota(jnp.int32, (c, c), 0)
    col = lax.broadcasted_iota(jnp.int32, (c, c), 1)
    mask = (row <= col) if reverse else (row >= col)
    ones = jnp.ones((c, LANES), BF16)
    for h in range(HEADS):
        ks = slice(h * dk, (h + 1) * dk)
        vs = slice(h * dv, (h + 1) * dv)
        g = GATE_LANE + d * HEADS + h
        qh = qk_ref[:, ks]
        kh = qk_ref[:, qk_w + h * dk:qk_w + (h + 1) * dk]
        e = jnp.exp2(jnp.where(mask, col_a[:, g:g + 1] + r_t[g:g + 1, :], -jnp.inf))
        s = (_dot_nt(qh, kh) * e).astype(BF16)
        iq = (qh.astype(F32) * inter[:, g:g + 1]).astype(BF16)
        v_aug = jnp.concatenate([v_ref[:, vs], ones], axis=1)
        sa = s_ref[h]
        out = _dot(jnp.concatenate([s, iq], axis=1),
                   jnp.concatenate([v_aug, sa.astype(BF16)], axis=0))
        rden = 1.0 / jnp.maximum(jnp.abs(out[:, dv:]), em[:, g:g + 1])
        h_ref[:, vs] = (out[:, :dv] * jnp.tile(rden, (1, dv // LANES))).astype(h_ref.dtype)
        kw = (kh.astype(F32) * kw_scale[:, g:g + 1]).astype(BF16)
        s_ref[h] = sa * decay[:, g:g + 1] + _dot_tn(kw, v_aug)


def _mlstm_kernel(qk_f, v_f, zs_f, qk_b, v_b, zs_b, bi, bf, h_f, h_b, s_f, m_f, s_b, m_b, *, dk, dv):
    @pl.when(pl.program_id(1) == 0)
    def _():
        s_f[...] = jnp.zeros_like(s_f)
        s_b[...] = jnp.zeros_like(s_b)
        m_f[...] = jnp.full_like(m_f, -jnp.inf)
        m_b[...] = jnp.full_like(m_b, -jnp.inf)

    _mlstm_direction(qk_f, v_f, zs_f, bi, bf, h_f, s_f, m_f, False, 0, dk, dv)
    _mlstm_direction(qk_b, v_b, zs_b, bi, bf, h_b, s_b, m_b, True, 1, dk, dv)


def _mlstm_scan(qk, zbig, zsmall, bias_i, bias_f, dk, dv, v_col):
    b, s, _ = zbig.shape
    c = ML_CHUNK
    nc = s // c
    qk_w = 2 * HEADS * dk
    v_w = HEADS * dv
    fwd = lambda n: n
    bwd = lambda n: nc - 1 - n
    specs = []
    for ch in (fwd, bwd):
        specs += [
            pl.BlockSpec((None, c, qk_w), lambda i, n, ch=ch: (i, ch(n), 0)),
            pl.BlockSpec((None, c, v_w), lambda i, n, ch=ch: (i, ch(n), v_col)),
            pl.BlockSpec((None, c, SMALL_W), lambda i, n, ch=ch: (i, ch(n), 0)),
        ]
    const = lambda i, n: (0, 0)
    specs += [pl.BlockSpec((1, LANES), const)] * 2
    state = [pltpu.VMEM((HEADS, dk, dv + LANES), F32), pltpu.VMEM((1, LANES), F32)]
    return pl.pallas_call(
        functools.partial(_mlstm_kernel, dk=dk, dv=dv),
        grid=(b, nc),
        in_specs=specs,
        out_specs=[
            pl.BlockSpec((None, c, v_w), lambda i, n: (i, n, 0)),
            pl.BlockSpec((None, c, v_w), lambda i, n: (i, nc - 1 - n, 0)),
        ],
        out_shape=[jax.ShapeDtypeStruct((b, s, v_w), BF16)] * 2,
        scratch_shapes=state + state,
        compiler_params=pltpu.CompilerParams(
            dimension_semantics=("arbitrary", "arbitrary"), vmem_limit_bytes=VMEM_LIMIT),
        name="mlstm_scan",
    )(qk, zbig, zsmall, qk, zbig, zsmall, bias_i, bias_f)


def _head_rms(o, w, dv):
    parts = []
    for h in range(HEADS):
        seg = o[:, h * dv:(h + 1) * dv]
        parts.append(seg * lax.rsqrt(jnp.mean(seg * seg, axis=-1, keepdims=True) + EPS))
    return jnp.concatenate(parts, axis=-1) * w


def _merge_kernel(x_ref, oaf, oab, hbf, hbb, gr, mo, ga, gb, gn, mn, pn, wo, out_ref, *, dv):
    o_a = oaf[...].astype(F32) + oab[...].astype(F32)
    h_b = hbf[...].astype(F32) + hbb[...].astype(F32)
    g_r = gr[...].astype(F32)
    y_a = _head_rms(o_a, gn[...], dv) * (g_r * _sigmoid(g_r))
    y_b = _head_rms(h_b, mn[...], dv) * _sigmoid(mo[...].astype(F32))
    y = _sigmoid(ga[...].astype(F32)) * y_a + _sigmoid(gb[...].astype(F32)) * y_b
    mix = _dot(y.astype(BF16), wo[...])
    out_ref[...] = x_ref[...] + _rms(mix, pn[...])


def _merge_out(x2d, o_af, o_ab, h_bf, h_bb, zbig, gla_norm, ml_norm, post_norm, w_out, tm, cols, dv):
    t, d = x2d.shape
    row = lambda i: (i, 0)
    const = lambda i: (0, 0)
    zspec = lambda cidx: pl.BlockSpec((tm, d), lambda i, cidx=cidx: (i, cidx))
    return pl.pallas_call(
        functools.partial(_merge_kernel, dv=dv),
        grid=(t // tm,),
        in_specs=[pl.BlockSpec((tm, d), row)] * 5
        + [zspec(cols["gr"]), zspec(cols["mo"]), zspec(cols["ga"]), zspec(cols["gb"])]
        + [pl.BlockSpec((1, d), const)] * 3
        + [pl.BlockSpec((d, d), const)],
        out_specs=pl.BlockSpec((tm, d), row),
        out_shape=jax.ShapeDtypeStruct((t, d), F32),
        compiler_params=pltpu.CompilerParams(
            dimension_semantics=("arbitrary",), vmem_limit_bytes=VMEM_LIMIT),
        name="merge_out",
    )(x2d, o_af, o_ab, h_bf, h_bb, zbig, zbig, zbig, zbig, gla_norm, ml_norm, post_norm, w_out)


_GELU_K1 = -2.0 * 0.7978845608028654 * LOG2E
_GELU_K2 = _GELU_K1 * 0.044715


def _gelu_tanh(x):
    return x / (1.0 + jnp.exp2(x * (x * x * _GELU_K2 + _GELU_K1)))


def _ffn_kernel(xp, xm, xn, p_ref, n_pre, n_post, wug, wuv, cwg, cwv, cbg, cbv, wd, wg, wp, n_ple,
                out_ref, h_scr, acc, *, tiles_per_seq):
    i = pl.program_id(0)
    j = pl.program_id(1)
    tm = xm.shape[0]
    n = tm + 2 * HALO

    @pl.when(j == 0)
    def _():
        t_in_seq = i % tiles_per_seq
        nw = n_pre[...]
        hp = jnp.where(t_in_seq > 0, _rms(xp[...], nw), 0.0)
        hn = jnp.where(t_in_seq < tiles_per_seq - 1, _rms(xn[...], nw), 0.0)
        h_scr[0:HALO, :] = hp.astype(BF16)
        h_scr[HALO:HALO + tm, :] = _rms(xm[...], nw).astype(BF16)
        h_scr[HALO + tm:n, :] = hn.astype(BF16)
        acc[...] = jnp.zeros_like(acc)

    h = h_scr[...]

    def conv(w_ref, cw_ref, cb_ref):
        u = _dot(h, w_ref[...])
        up = pltpu.roll(u, 1, axis=0)[HALO:HALO + tm]
        dn = pltpu.roll(u, n - 1, axis=0)[HALO:HALO + tm]
        return (up * cw_ref[0:1, :] + u[HALO:HALO + tm] * cw_ref[1:2, :] + dn * cw_ref[2:3, :]
                + cb_ref[...])

    act = _gelu_tanh(conv(wug, cwg, cbg)) * conv(wuv, cwv, cbv)
    acc[...] += _dot(act.astype(BF16), wd[...])

    @pl.when(j == pl.num_programs(1) - 1)
    def _():
        x2 = xm[...] + _rms(acc[...], n_post[...])
        gate = _sigmoid(_dot(x2.astype(BF16), wg[...]))
        e = _dot(p_ref[...].astype(BF16), wp[...])
        out_ref[...] = x2 + _rms(gate * e, n_ple[...])


def _ffn_ple(x1, p2d, n_pre, n_post, w_up, conv_w, conv_b, w_down, w_gate, w_proj, n_ple, seq, tm, fb):
    t, d = x1.shape
    dff = w_down.shape[0]
    nj = dff // fb
    hb = tm // HALO
    nh = t // HALO
    pd = p2d.shape[1]
    const = lambda i, j: (0, 0)
    row = lambda i, j: (i, 0)
    return pl.pallas_call(
        functools.partial(_ffn_kernel, tiles_per_seq=seq // tm),
        grid=(t // tm, nj),
        in_specs=[
            pl.BlockSpec((HALO, d), lambda i, j: (jnp.maximum(i * hb - 1, 0), 0)),
            pl.BlockSpec((tm, d), row),
            pl.BlockSpec((HALO, d), lambda i, j: (jnp.minimum((i + 1) * hb, nh - 1), 0)),
            pl.BlockSpec((tm, pd), row),
            pl.BlockSpec((1, d), const),
            pl.BlockSpec((1, d), const),
            pl.BlockSpec((d, fb), lambda i, j: (0, j)),
            pl.BlockSpec((d, fb), lambda i, j: (0, nj + j)),
            pl.BlockSpec((3, fb), lambda i, j: (0, j)),
            pl.BlockSpec((3, fb), lambda i, j: (0, nj + j)),
            pl.BlockSpec((1, fb), lambda i, j: (0, j)),
            pl.BlockSpec((1, fb), lambda i, j: (0, nj + j)),
            pl.BlockSpec((fb, d), lambda i, j: (j, 0)),
            pl.BlockSpec((d, d), const),
            pl.BlockSpec((pd, d), const),
            pl.BlockSpec((1, d), const),
        ],
        out_specs=pl.BlockSpec((tm, d), row),
        out_shape=jax.ShapeDtypeStruct((t, d), F32),
        scratch_shapes=[pltpu.VMEM((tm + 2 * HALO, d), BF16), pltpu.VMEM((tm, d), F32)],
        compiler_params=pltpu.CompilerParams(
            dimension_semantics=("arbitrary", "arbitrary"), vmem_limit_bytes=VMEM_LIMIT),
        name="ffn_ple",
    )(x1, x1, x1, p2d, n_pre, n_post, w_up, w_up, conv_w, conv_w, conv_b, conv_b, w_down,
      w_gate, w_proj, n_ple)


def _layer(x, p, norm_mix_pre, norm_mix_post, w_in, w_gla_decay_f, b_gla_decay_f, w_gla_decay_b,
           b_gla_decay_b, gla_norm, ml_conv_w, ml_conv_b, ml_igate_b, ml_fgate_b, ml_norm, w_out,
           norm_ffn_pre, norm_ffn_post, w_up, ffn_conv_w, ffn_conv_b, w_down, w_ple_gate, w_ple_proj,
           norm_ple_post):
    bsz, seq, d = x.shape
    t = bsz * seq
    qk_w = d // 2
    dk = qk_w // HEADS
    dv = d // HEADS
    row = lambda a: a.reshape(1, -1)

    o_glr = 2 * qk_w + 2 * d
    o_mqk = o_glr + 2 * GLA_RANK
    o_mi = o_mqk + 2 * qk_w + 2 * d
    o_mf = o_mi + 2 * HEADS
    o_ga = o_mf + 2 * HEADS
    w_big = jnp.concatenate([w_in[:, :o_glr], w_in[:, o_mqk:o_mi], w_in[:, o_ga:]], axis=1).astype(BF16)
    n_gate = 2 * HEADS
    zeros = lambda n: jnp.zeros((d, n), F32)
    w_small = jnp.concatenate(
        [w_in[:, o_glr:o_mqk], w_in[:, o_mi:o_mf], zeros(LANES - GATE_LANE - n_gate),
         zeros(GATE_LANE), w_in[:, o_mf:o_ga], zeros(LANES - GATE_LANE - n_gate)], axis=1).astype(BF16)
    cols = {"gr": 2, "mo": 5, "ga": 6, "gb": 7}

    x2d = x.reshape(t, d)
    zbig, zsmall = _in_proj(x2d, row(norm_mix_pre), w_big, w_small, tm=1024, tn=1024)
    zbig3 = zbig.reshape(bsz, seq, -1)
    zsmall3 = zsmall.reshape(bsz, seq, SMALL_W)

    def pad_decay(w, off):
        return jnp.zeros((LANES, qk_w), F32).at[off:off + GLA_RANK].set(w).astype(BF16)

    o_af, o_ab = _gla_scan(zbig3, zsmall3, pad_decay(w_gla_decay_f, 0), row(b_gla_decay_f),
                           pad_decay(w_gla_decay_b, GLA_RANK), row(b_gla_decay_b), dk, dv)

    def gate_bias(bvec):
        return jnp.zeros((1, LANES), F32).at[0, GATE_LANE:GATE_LANE + n_gate].set(bvec)

    qk = _qk_conv(zbig3, ml_conv_w, row(ml_conv_b), dk, qk_col=3, tc=512)
    h_bf, h_bb = _mlstm_scan(qk, zbig3, zsmall3, gate_bias(ml_igate_b), gate_bias(ml_fgate_b),
                             dk, dv, v_col=4)

    x1 = _merge_out(x2d, o_af.reshape(t, d), o_ab.reshape(t, d), h_bf.reshape(t, d), h_bb.reshape(t, d),
                    zbig, row(gla_norm), row(ml_norm), row(norm_mix_post), w_out.astype(BF16),
                    tm=512, cols=cols, dv=dv)

    out = _ffn_ple(x1, p.reshape(t, -1), row(norm_ffn_pre), row(norm_ffn_post), w_up.astype(BF16),
                   ffn_conv_w, row(ffn_conv_b), w_down.astype(BF16), w_ple_gate.astype(BF16),
                   w_ple_proj.astype(BF16), row(norm_ple_post), seq=seq, tm=512, fb=512)
    return out.reshape(bsz, seq, d)


def kernel(x, p, norm_mix_pre, norm_mix_post, w_in, w_gla_decay_f, b_gla_decay_f, w_gla_decay_b, b_gla_decay_b, gla_norm, ml_conv_w, ml_conv_b, ml_igate_b, ml_fgate_b, ml_norm, w_out, norm_ffn_pre, norm_ffn_post, w_up, ffn_conv_w, ffn_conv_b, w_down, w_ple_gate, w_ple_proj, norm_ple_post):
    depth = w_in.shape[0]
    for i in range(depth):
        x = _layer(x, p[i], norm_mix_pre[i], norm_mix_post[i], w_in[i], w_gla_decay_f[i], b_gla_decay_f[i],
                   w_gla_decay_b[i], b_gla_decay_b[i], gla_norm[i], ml_conv_w[i], ml_conv_b[i], ml_igate_b[i],
                   ml_fgate_b[i], ml_norm[i], w_out[i], norm_ffn_pre[i], norm_ffn_post[i], w_up[i],
                   ffn_conv_w[i], ffn_conv_b[i], w_down[i], w_ple_gate[i], w_ple_proj[i], norm_ple_post[i])
    return x
```

```python
import functools
import math

import jax
import jax.numpy as jnp
from jax import lax
from jax.experimental import pallas as pl
from jax.experimental.pallas import tpu as pltpu

EPS = 1e-6
HEADS = 4
GLA_RANK = 16
GLA_GATE_NORM = 16.0
GLA_CHUNK = 64
ML_CHUNK = 128
GLA_ROWS = 256
ML_ROWS = 256
HALO = 16
LANES = 128
SUBLANES = 8
SMALL_W = 2 * LANES
GATE_LANE = 2 * GLA_RANK
VMEM_LIMIT = 56 * 1024 * 1024
LOG2E = math.log2(math.e)

F32 = jnp.float32
BF16 = jnp.bfloat16

_NT = (((1,), (1,)), ((), ()))
_TN = (((0,), (0,)), ((), ()))


def _dot(a, b):
    return jnp.dot(a, b, preferred_element_type=F32)


def _dot_nt(a, b):
    return lax.dot_general(a, b, _NT, preferred_element_type=F32)


def _dot_tn(a, b):
    return lax.dot_general(a, b, _TN, preferred_element_type=F32)


def _rms(x, w):
    return x * lax.rsqrt(jnp.mean(x * x, axis=-1, keepdims=True) + EPS) * w


def _sigmoid(x):
    return 1.0 / (1.0 + jnp.exp2(x * (-LOG2E)))


def _log2_sigmoid(x):
    y = x * LOG2E
    return jnp.minimum(y, 0.0) - jnp.log2(1.0 + jnp.exp2(-jnp.abs(y)))


def _scan_rows(x, op, fill, reverse):
    n = x.shape[0]
    row = lax.broadcasted_iota(jnp.int32, x.shape, 0)
    s = 1
    while s < min(n, SUBLANES):
        if reverse:
            x = op(x, jnp.where(row < n - s, pltpu.roll(x, n - s, axis=0), fill))
        else:
            x = op(x, jnp.where(row >= s, pltpu.roll(x, s, axis=0), fill))
        s *= 2
    while s < n:
        if reverse:
            x = jnp.concatenate([op(x[:n - s], x[s:]), x[n - s:]], axis=0)
        else:
            x = jnp.concatenate([x[:s], op(x[s:], x[:n - s])], axis=0)
        s *= 2
    return x


def _conv3(prev_ref, main_ref, next_ref, w_ref, b_ref, has_prev, has_next):
    c = main_ref.shape[0]
    prev = jnp.where(has_prev, prev_ref[...].astype(F32), 0.0)
    nxt = jnp.where(has_next, next_ref[...].astype(F32), 0.0)
    ext = jnp.concatenate([prev, main_ref[...].astype(F32), nxt], axis=0)
    n = ext.shape[0]
    up = pltpu.roll(ext, 1, axis=0)[HALO:HALO + c]
    dn = pltpu.roll(ext, n - 1, axis=0)[HALO:HALO + c]
    return up * w_ref[0:1, :] + ext[HALO:HALO + c] * w_ref[1:2, :] + dn * w_ref[2:3, :] + b_ref[...]


def _in_proj_kernel(x_ref, nw_ref, wbig_ref, wsmall_ref, zbig_ref, zsmall_ref, h_scr):
    @pl.when(pl.program_id(1) == 0)
    def _():
        h = _rms(x_ref[...], nw_ref[...]).astype(BF16)
        h_scr[...] = h
        zsmall_ref[...] = _dot(h, wsmall_ref[...])

    zbig_ref[...] = _dot(h_scr[...], wbig_ref[...]).astype(BF16)


def _in_proj(x2d, nw, w_big, w_small, tm, tn):
    t, d = x2d.shape
    nbig = w_big.shape[1]
    return pl.pallas_call(
        _in_proj_kernel,
        grid=(t // tm, nbig // tn),
        in_specs=[
            pl.BlockSpec((tm, d), lambda i, j: (i, 0)),
            pl.BlockSpec((1, d), lambda i, j: (0, 0)),
            pl.BlockSpec((d, tn), lambda i, j: (0, j)),
            pl.BlockSpec((d, SMALL_W), lambda i, j: (0, 0)),
        ],
        out_specs=[
            pl.BlockSpec((tm, tn), lambda i, j: (i, j)),
            pl.BlockSpec((tm, SMALL_W), lambda i, j: (i, 0)),
        ],
        out_shape=[
            jax.ShapeDtypeStruct((t, nbig), BF16),
            jax.ShapeDtypeStruct((t, SMALL_W), F32),
        ],
        scratch_shapes=[pltpu.VMEM((tm, d), BF16)],
        compiler_params=pltpu.CompilerParams(
            dimension_semantics=("arbitrary", "arbitrary"), vmem_limit_bytes=VMEM_LIMIT),
        name="in_proj",
    )(x2d, nw, w_big, w_small)


def _qk_conv_kernel(prev_ref, main_ref, next_ref, w_ref, b_ref, out_ref, *, dk):
    n = pl.program_id(1)
    y = _conv3(prev_ref, main_ref, next_ref, w_ref, b_ref, n > 0, n < pl.num_programs(1) - 1)
    y = y * _sigmoid(y)
    half = y.shape[1] // 2
    out_ref[:, :half] = y[:, :half].astype(out_ref.dtype)
    out_ref[:, half:] = (y[:, half:] * (dk ** -0.5)).astype(out_ref.dtype)


def _qk_conv(zbig3, conv_w, conv_b, dk, qk_col, tc):
    b, s, _ = zbig3.shape
    w = conv_w.shape[1]
    hb = tc // HALO
    nh = s // HALO
    const = lambda i, n: (0, 0)
    return pl.pallas_call(
        functools.partial(_qk_conv_kernel, dk=dk),
        grid=(b, s // tc),
        in_specs=[
            pl.BlockSpec((None, HALO, w), lambda i, n: (i, jnp.maximum(n * hb - 1, 0), qk_col)),
            pl.BlockSpec((None, tc, w), lambda i, n: (i, n, qk_col)),
            pl.BlockSpec((None, HALO, w), lambda i, n: (i, jnp.minimum((n + 1) * hb, nh - 1), qk_col)),
            pl.BlockSpec((3, w), const),
            pl.BlockSpec((1, w), const),
        ],
        out_specs=pl.BlockSpec((None, tc, w), lambda i, n: (i, n, 0)),
        out_shape=jax.ShapeDtypeStruct((b, s, w), BF16),
        compiler_params=pltpu.CompilerParams(
            dimension_semantics=("arbitrary", "arbitrary"), vmem_limit_bytes=VMEM_LIMIT),
        name="qk_conv",
    )(zbig3, zbig3, zbig3, conv_w, conv_b)


def _gla_gates(qkv_ref, pre, rows, reverse, dk):
    c = rows.stop - rows.start
    qk_w = HEADS * dk
    b2 = _scan_rows(_log2_sigmoid(pre[rows, :]) * (1.0 / GLA_GATE_NORM), jnp.add, 0.0, reverse)
    eb = jnp.exp2(b2)
    qe = (qkv_ref[rows, 0:qk_w].astype(F32) * (eb * (dk ** -0.5))).astype(BF16)
    ke = (qkv_ref[rows, qk_w:2 * qk_w].astype(F32) * jnp.exp2(-b2)).astype(BF16)
    end = 0 if reverse else c - 1
    return qe, ke, eb[end:end + 1, :]


def _gla_kernel(qkv_f, qkv_b, zs_f, zs_b, wdf, bdf, wdb, bdb, o_f, o_b, st_f, st_b, *, dk, dv):
    @pl.when(pl.program_id(1) == 0)
    def _():
        st_f[...] = jnp.zeros_like(st_f)
        st_b[...] = jnp.zeros_like(st_b)

    r = qkv_f.shape[0]
    c = GLA_CHUNK
    qk_w = HEADS * dk
    row = lax.broadcasted_iota(jnp.int32, (c, c), 0)
    col = lax.broadcasted_iota(jnp.int32, (c, c), 1)
    dirs = [(qkv_f, _dot(zs_f[...].astype(BF16), wdf[...]) + bdf[...], o_f, st_f, False, row >= col),
            (qkv_b, _dot(zs_b[...].astype(BF16), wdb[...]) + bdb[...], o_b, st_b, True, row <= col)]
    for lvl in range(r // c):
        units = []
        for qkv, pre, o_ref, st_ref, reverse, mask in dirs:
            i = r - c - lvl * c if reverse else lvl * c
            rows = slice(i, i + c)
            qe, ke, dec = _gla_gates(qkv, pre, rows, reverse, dk)
            for h in range(HEADS):
                ks = slice(h * dk, (h + 1) * dk)
                units.append(dict(
                    qh=qe[:, ks], kh=ke[:, ks], dec=dec[:, ks], mask=mask,
                    vh=qkv[rows, 2 * qk_w + h * dv:2 * qk_w + (h + 1) * dv],
                    st_ref=st_ref, h=h, o_ref=o_ref, rows=rows, vs=slice(h * dv, (h + 1) * dv)))
        scores = [_dot_nt(u["qh"], u["kh"]) for u in units]
        states = [u["st_ref"][u["h"]] for u in units]
        inter = [_dot_nt(u["qh"], st.astype(BF16)) for u, st in zip(units, states)]
        kv = [_dot_tn(u["vh"], u["kh"]) for u in units]
        a = [jnp.where(u["mask"], s, 0.0).astype(BF16) for u, s in zip(units, scores)]
        for u, a_u, inter_u in zip(units, a, inter):
            u["o_ref"][u["rows"], u["vs"]] = (_dot(a_u, u["vh"]) + inter_u).astype(u["o_ref"].dtype)
        for u, st, kv_u in zip(units, states, kv):
            u["st_ref"][u["h"]] = (st + kv_u) * u["dec"]


def _gla_scan(zbig, zsmall, wdf, bdf, wdb, bdb, dk, dv, rows):
    b, s, _ = zbig.shape
    nb = s // rows
    qkv_w = HEADS * (2 * dk + dv)
    v_w = HEADS * dv
    fwd = lambda i, n: (i, n, 0)
    bwd = lambda i, n: (i, nb - 1 - n, 0)
    const = lambda i, n: (0, 0)
    return pl.pallas_call(
        functools.partial(_gla_kernel, dk=dk, dv=dv),
        grid=(b, nb),
        in_specs=[
            pl.BlockSpec((None, rows, qkv_w), fwd),
            pl.BlockSpec((None, rows, qkv_w), bwd),
            pl.BlockSpec((None, rows, LANES), fwd),
            pl.BlockSpec((None, rows, LANES), bwd),
            pl.BlockSpec((LANES, HEADS * dk), const),
            pl.BlockSpec((1, HEADS * dk), const),
            pl.BlockSpec((LANES, HEADS * dk), const),
            pl.BlockSpec((1, HEADS * dk), const),
        ],
        out_specs=[
            pl.BlockSpec((None, rows, v_w), fwd),
            pl.BlockSpec((None, rows, v_w), bwd),
        ],
        out_shape=[jax.ShapeDtypeStruct((b, s, v_w), BF16)] * 2,
        scratch_shapes=[pltpu.VMEM((HEADS, dv, dk), F32)] * 2,
        compiler_params=pltpu.CompilerParams(
            dimension_semantics=("arbitrary", "arbitrary"), vmem_limit_bytes=VMEM_LIMIT),
        name="gla_scan",
    )(zbig, zbig, zsmall, zsmall, wdf, bdf, wdb, bdb)


def _mlstm_chunk(qk_ref, v_ref, zs_ref, bi_ref, bf_ref, h_ref, s_ref, m_ref, rows, reverse, d, dk, dv):
    c = rows.stop - rows.start
    qk_w = HEADS * dk
    zs = zs_ref[rows, :]
    pre_i = (zs[:, :LANES] + bi_ref[...]) * LOG2E
    f = _scan_rows(_log2_sigmoid(zs[:, LANES:] + bf_ref[...]), jnp.add, 0.0, reverse)
    r = pre_i - f
    cm = _scan_rows(r, jnp.maximum, -jnp.inf, reverse)
    m_st = m_ref[...]
    inter_log = f + m_st
    m_i = jnp.maximum(inter_log, f + cm)
    col_a = f - m_i
    inter = jnp.exp2(inter_log - m_i)
    em = jnp.exp2(-m_i)
    last = 0 if reverse else c - 1
    f_end = f[last:last + 1, :]
    m_new = f_end + jnp.maximum(m_st, cm[last:last + 1, :])
    decay = jnp.exp2(f_end + m_st - m_new)
    kw_scale = jnp.exp2(f_end + r - m_new)
    m_ref[...] = m_new
    r_t = r.T

    row = lax.broadcasted_iota(jnp.int32, (c, c), 0)
    col = lax.broadcasted_iota(jnp.int32, (c, c), 1)
    mask = (row <= col) if reverse else (row >= col)
    ones = jnp.ones((c, LANES), BF16)
    units = []
    for h in range(HEADS):
        g = GATE_LANE + d * HEADS + h
        units.append(dict(
            qh=qk_ref[rows, h * dk:(h + 1) * dk], kh=qk_ref[rows, qk_w + h * dk:qk_w + (h + 1) * dk],
            v_aug=jnp.concatenate([v_ref[rows, h * dv:(h + 1) * dv], ones], axis=1),
            col_a=col_a[:, g:g + 1], r_row=r_t[g:g + 1, :], inter=inter[:, g:g + 1],
            em=em[:, g:g + 1], kw_scale=kw_scale[:, g:g + 1], decay=decay[:, g:g + 1], mask=mask,
            s_ref=s_ref, h=h, h_ref=h_ref, rows=rows, vs=slice(h * dv, (h + 1) * dv)))
    return units


def _mlstm_level(units, dv):
    scores = [_dot_nt(u["qh"], u["kh"]) for u in units]
    states = [u["s_ref"][u["h"]] for u in units]
    upd = [_dot_tn((u["kh"].astype(F32) * u["kw_scale"]).astype(BF16), u["v_aug"]) for u in units]
    outs = []
    for u, sc, sa in zip(units, scores, states):
        e = jnp.exp2(jnp.where(u["mask"], u["col_a"] + u["r_row"], -jnp.inf))
        s = (sc * e).astype(BF16)
        iq = (u["qh"].astype(F32) * u["inter"]).astype(BF16)
        outs.append(_dot(jnp.concatenate([s, iq], axis=1),
                         jnp.concatenate([u["v_aug"], sa.astype(BF16)], axis=0)))
    for u, out in zip(units, outs):
        rden = 1.0 / jnp.maximum(jnp.abs(out[:, dv:]), u["em"])
        u["h_ref"][u["rows"], u["vs"]] = (out[:, :dv] * jnp.tile(rden, (1, dv // LANES))).astype(
            u["h_ref"].dtype)
    for u, sa, up in zip(units, states, upd):
        u["s_ref"][u["h"]] = sa * u["decay"] + up


def _mlstm_kernel(qk_f, v_f, zs_f, qk_b, v_b, zs_b, bi, bf, h_f, h_b, s_f, m_f, s_b, m_b, *, dk, dv):
    @pl.when(pl.program_id(1) == 0)
    def _():
        s_f[...] = jnp.zeros_like(s_f)
        s_b[...] = jnp.zeros_like(s_b)
        m_f[...] = jnp.full_like(m_f, -jnp.inf)
        m_b[...] = jnp.full_like(m_b, -jnp.inf)

    r, c = v_f.shape[0], ML_CHUNK
    for i in range(0, r, c):
        j = r - c - i
        units = (_mlstm_chunk(qk_f, v_f, zs_f, bi, bf, h_f, s_f, m_f, slice(i, i + c), False, 0, dk, dv)
                 + _mlstm_chunk(qk_b, v_b, zs_b, bi, bf, h_b, s_b, m_b, slice(j, j + c), True, 1, dk, dv))
        _mlstm_level(units, dv)


def _mlstm_scan(qk, zbig, zsmall, bias_i, bias_f, dk, dv, v_col, rows):
    b, s, _ = zbig.shape
    nb = s // rows
    qk_w = 2 * HEADS * dk
    v_w = HEADS * dv
    fwd = lambda n: n
    bwd = lambda n: nb - 1 - n
    specs = []
    for ch in (fwd, bwd):
        specs += [
            pl.BlockSpec((None, rows, qk_w), lambda i, n, ch=ch: (i, ch(n), 0)),
            pl.BlockSpec((None, rows, v_w), lambda i, n, ch=ch: (i, ch(n), v_col)),
            pl.BlockSpec((None, rows, SMALL_W), lambda i, n, ch=ch: (i, ch(n), 0)),
        ]
    const = lambda i, n: (0, 0)
    specs += [pl.BlockSpec((1, LANES), const)] * 2
    state = [pltpu.VMEM((HEADS, dk, dv + LANES), F32), pltpu.VMEM((1, LANES), F32)]
    return pl.pallas_call(
        functools.partial(_mlstm_kernel, dk=dk, dv=dv),
        grid=(b, nb),
        in_specs=specs,
        out_specs=[
            pl.BlockSpec((None, rows, v_w), lambda i, n: (i, n, 0)),
            pl.BlockSpec((None, rows, v_w), lambda i, n: (i, nb - 1 - n, 0)),
        ],
        out_shape=[jax.ShapeDtypeStruct((b, s, v_w), BF16)] * 2,
        scratch_shapes=state + state,
        compiler_params=pltpu.CompilerParams(
            dimension_semantics=("arbitrary", "arbitrary"), vmem_limit_bytes=VMEM_LIMIT),
        name="mlstm_scan",
    )(qk, zbig, zsmall, qk, zbig, zsmall, bias_i, bias_f)


def _head_rms(o, w, dv):
    parts = []
    for h in range(HEADS):
        seg = o[:, h * dv:(h + 1) * dv]
        parts.append(seg * lax.rsqrt(jnp.mean(seg * seg, axis=-1, keepdims=True) + EPS))
    return jnp.concatenate(parts, axis=-1) * w


def _merge_kernel(x_ref, oaf, oab, hbf, hbb, gr, mo, ga, gb, gn, mn, pn, wo, out_ref, *, dv):
    o_a = oaf[...].astype(F32) + oab[...].astype(F32)
    h_b = hbf[...].astype(F32) + hbb[...].astype(F32)
    g_r = gr[...].astype(F32)
    y_a = _head_rms(o_a, gn[...], dv) * (g_r * _sigmoid(g_r))
    y_b = _head_rms(h_b, mn[...], dv) * _sigmoid(mo[...].astype(F32))
    y = _sigmoid(ga[...].astype(F32)) * y_a + _sigmoid(gb[...].astype(F32)) * y_b
    mix = _dot(y.astype(BF16), wo[...])
    out_ref[...] = x_ref[...] + _rms(mix, pn[...])


def _merge_out(x2d, o_af, o_ab, h_bf, h_bb, zbig, gla_norm, ml_norm, post_norm, w_out, tm, cols, dv):
    t, d = x2d.shape
    row = lambda i: (i, 0)
    const = lambda i: (0, 0)
    zspec = lambda cidx: pl.BlockSpec((tm, d), lambda i, cidx=cidx: (i, cidx))
    return pl.pallas_call(
        functools.partial(_merge_kernel, dv=dv),
        grid=(t // tm,),
        in_specs=[pl.BlockSpec((tm, d), row)] * 5
        + [zspec(cols["gr"]), zspec(cols["mo"]), zspec(cols["ga"]), zspec(cols["gb"])]
        + [pl.BlockSpec((1, d), const)] * 3
        + [pl.BlockSpec((d, d), const)],
        out_specs=pl.BlockSpec((tm, d), row),
        out_shape=jax.ShapeDtypeStruct((t, d), F32),
        compiler_params=pltpu.CompilerParams(
            dimension_semantics=("arbitrary",), vmem_limit_bytes=VMEM_LIMIT),
        name="merge_out",
    )(x2d, o_af, o_ab, h_bf, h_bb, zbig, zbig, zbig, zbig, gla_norm, ml_norm, post_norm, w_out)


_GELU_K1 = -2.0 * 0.7978845608028654 * LOG2E
_GELU_K2 = _GELU_K1 * 0.044715


def _gelu_tanh(x):
    return x / (1.0 + jnp.exp2(x * (x * x * _GELU_K2 + _GELU_K1)))


def _ffn_kernel(xp, xm, xn, p_ref, n_pre, n_post, wug, wuv, cwg, cwv, cbg, cbv, wd, wg, wp, n_ple,
                out_ref, h_scr, acc, *, tiles_per_seq):
    i = pl.program_id(0)
    j = pl.program_id(1)
    tm = xm.shape[0]
    n = tm + 2 * HALO

    @pl.when(j == 0)
    def _():
        t_in_seq = i % tiles_per_seq
        nw = n_pre[...]
        hp = jnp.where(t_in_seq > 0, _rms(xp[...], nw), 0.0)
        hn = jnp.where(t_in_seq < tiles_per_seq - 1, _rms(xn[...], nw), 0.0)
        h_scr[0:HALO, :] = hp.astype(BF16)
        h_scr[HALO:HALO + tm, :] = _rms(xm[...], nw).astype(BF16)
        h_scr[HALO + tm:n, :] = hn.astype(BF16)
        acc[...] = jnp.zeros_like(acc)

    h = h_scr[...]

    def conv(w_ref, cw_ref, cb_ref):
        u = _dot(h, w_ref[...])
        up = pltpu.roll(u, 1, axis=0)[HALO:HALO + tm]
        dn = pltpu.roll(u, n - 1, axis=0)[HALO:HALO + tm]
        return (up * cw_ref[0:1, :] + u[HALO:HALO + tm] * cw_ref[1:2, :] + dn * cw_ref[2:3, :]
                + cb_ref[...])

    act = _gelu_tanh(conv(wug, cwg, cbg)) * conv(wuv, cwv, cbv)
    acc[...] += _dot(act.astype(BF16), wd[...])

    @pl.when(j == pl.num_programs(1) - 1)
    def _():
        x2 = xm[...] + _rms(acc[...], n_post[...])
        gate = _sigmoid(_dot(x2.astype(BF16), wg[...]))
        e = _dot(p_ref[...].astype(BF16), wp[...])
        out_ref[...] = x2 + _rms(gate * e, n_ple[...])


def _ffn_ple(x1, p2d, n_pre, n_post, w_up, conv_w, conv_b, w_down, w_gate, w_proj, n_ple, seq, tm, fb):
    t, d = x1.shape
    dff = w_down.shape[0]
    nj = dff // fb
    hb = tm // HALO
    nh = t // HALO
    pd = p2d.shape[1]
    const = lambda i, j: (0, 0)
    row = lambda i, j: (i, 0)
    return pl.pallas_call(
        functools.partial(_ffn_kernel, tiles_per_seq=seq // tm),
        grid=(t // tm, nj),
        in_specs=[
            pl.BlockSpec((HALO, d), lambda i, j: (jnp.maximum(i * hb - 1, 0), 0)),
            pl.BlockSpec((tm, d), row),
            pl.BlockSpec((HALO, d), lambda i, j: (jnp.minimum((i + 1) * hb, nh - 1), 0)),
            pl.BlockSpec((tm, pd), row),
            pl.BlockSpec((1, d), const),
            pl.BlockSpec((1, d), const),
            pl.BlockSpec((d, fb), lambda i, j: (0, j)),
            pl.BlockSpec((d, fb), lambda i, j: (0, nj + j)),
            pl.BlockSpec((3, fb), lambda i, j: (0, j)),
            pl.BlockSpec((3, fb), lambda i, j: (0, nj + j)),
            pl.BlockSpec((1, fb), lambda i, j: (0, j)),
            pl.BlockSpec((1, fb), lambda i, j: (0, nj + j)),
            pl.BlockSpec((fb, d), lambda i, j: (j, 0)),
            pl.BlockSpec((d, d), const),
            pl.BlockSpec((pd, d), const),
            pl.BlockSpec((1, d), const),
        ],
        out_specs=pl.BlockSpec((tm, d), row),
        out_shape=jax.ShapeDtypeStruct((t, d), F32),
        scratch_shapes=[pltpu.VMEM((tm + 2 * HALO, d), BF16), pltpu.VMEM((tm, d), F32)],
        compiler_params=pltpu.CompilerParams(
            dimension_semantics=("arbitrary", "arbitrary"), vmem_limit_bytes=VMEM_LIMIT),
        name="ffn_ple",
    )(x1, x1, x1, p2d, n_pre, n_post, w_up, w_up, conv_w, conv_w, conv_b, conv_b, w_down,
      w_gate, w_proj, n_ple)


def _layer(x, p, norm_mix_pre, norm_mix_post, w_in, w_gla_decay_f, b_gla_decay_f, w_gla_decay_b,
           b_gla_decay_b, gla_norm, ml_conv_w, ml_conv_b, ml_igate_b, ml_fgate_b, ml_norm, w_out,
           norm_ffn_pre, norm_ffn_post, w_up, ffn_conv_w, ffn_conv_b, w_down, w_ple_gate, w_ple_proj,
           norm_ple_post):
    bsz, seq, d = x.shape
    t = bsz * seq
    qk_w = d // 2
    dk = qk_w // HEADS
    dv = d // HEADS
    row = lambda a: a.reshape(1, -1)

    o_glr = 2 * qk_w + 2 * d
    o_mqk = o_glr + 2 * GLA_RANK
    o_mi = o_mqk + 2 * qk_w + 2 * d
    o_mf = o_mi + 2 * HEADS
    o_ga = o_mf + 2 * HEADS
    w_big = jnp.concatenate([w_in[:, :o_glr], w_in[:, o_mqk:o_mi], w_in[:, o_ga:]], axis=1).astype(BF16)
    n_gate = 2 * HEADS
    zeros = lambda n: jnp.zeros((d, n), F32)
    w_small = jnp.concatenate(
        [w_in[:, o_glr:o_mqk], w_in[:, o_mi:o_mf], zeros(LANES - GATE_LANE - n_gate),
         zeros(GATE_LANE), w_in[:, o_mf:o_ga], zeros(LANES - GATE_LANE - n_gate)], axis=1).astype(BF16)
    cols = {"gr": 2, "mo": 5, "ga": 6, "gb": 7}

    x2d = x.reshape(t, d)
    zbig, zsmall = _in_proj(x2d, row(norm_mix_pre), w_big, w_small, tm=1024, tn=1024)
    zbig3 = zbig.reshape(bsz, seq, -1)
    zsmall3 = zsmall.reshape(bsz, seq, SMALL_W)

    def pad_decay(w, off):
        return jnp.zeros((LANES, qk_w), F32).at[off:off + GLA_RANK].set(w).astype(BF16)

    o_af, o_ab = _gla_scan(zbig3, zsmall3, pad_decay(w_gla_decay_f, 0), row(b_gla_decay_f),
                           pad_decay(w_gla_decay_b, GLA_RANK), row(b_gla_decay_b), dk, dv, rows=GLA_ROWS)

    def gate_bias(bvec):
        return jnp.zeros((1, LANES), F32).at[0, GATE_LANE:GATE_LANE + n_gate].set(bvec)

    qk = _qk_conv(zbig3, ml_conv_w, row(ml_conv_b), dk, qk_col=3, tc=512)
    h_bf, h_bb = _mlstm_scan(qk, zbig3, zsmall3, gate_bias(ml_igate_b), gate_bias(ml_fgate_b),
                             dk, dv, v_col=4, rows=ML_ROWS)

    x1 = _merge_out(x2d, o_af.reshape(t, d), o_ab.reshape(t, d), h_bf.reshape(t, d), h_bb.reshape(t, d),
                    zbig, row(gla_norm), row(ml_norm), row(norm_mix_post), w_out.astype(BF16),
                    tm=512, cols=cols, dv=dv)

    out = _ffn_ple(x1, p.reshape(t, -1), row(norm_ffn_pre), row(norm_ffn_post), w_up.astype(BF16),
                   ffn_conv_w, row(ffn_conv_b), w_down.astype(BF16), w_ple_gate.astype(BF16),
                   w_ple_proj.astype(BF16), row(norm_ple_post), seq=seq, tm=512, fb=512)
    return out.reshape(bsz, seq, d)


def kernel(x, p, norm_mix_pre, norm_mix_post, w_in, w_gla_decay_f, b_gla_decay_f, w_gla_decay_b, b_gla_decay_b, gla_norm, ml_conv_w, ml_conv_b, ml_igate_b, ml_fgate_b, ml_norm, w_out, norm_ffn_pre, norm_ffn_post, w_up, ffn_conv_w, ffn_conv_b, w_down, w_ple_gate, w_ple_proj, norm_ple_post):
    depth = w_in.shape[0]
    for i in range(depth):
        x = _layer(x, p[i], norm_mix_pre[i], norm_mix_post[i], w_in[i], w_gla_decay_f[i], b_gla_decay_f[i],
                   w_gla_decay_b[i], b_gla_decay_b[i], gla_norm[i], ml_conv_w[i], ml_conv_b[i], ml_igate_b[i],
                   ml_fgate_b[i], ml_norm[i], w_out[i], norm_ffn_pre[i], norm_ffn_post[i], w_up[i],
                   ffn_conv_w[i], ffn_conv_b[i], w_down[i], w_ple_gate[i], w_ple_proj[i], norm_ple_post[i])
    return x
```
